```python
import math
import jax, jax.numpy as jnp
from jax import lax
import numpy as np

D_MODEL = 1024
BATCH = 8
SEQ = 2048
DEPTH = 4

MEM_LEN = 256
CONV_W = 512
CONV_WIDTH = 3
ATT_HEADS = 8
ATT_DH = 64
ATT_W = ATT_HEADS * ATT_DH
MOBA_BLOCK = 256
MOBA_TOPK = 3
Q_CHUNK = 16
MEM_HEADS = 4
MEM_DH = 128
MEM_W = MEM_HEADS * MEM_DH
NUM_BUCKETS = 32
MAX_EXACT = NUM_BUCKETS // 2
MAX_DISTANCE = 1024
D_FF = 2816
FFN_CONV_WIDTH = 3
N_BRANCHES = 3
RMS_EPS = 1e-6
NEG = -1e30

IN_SPLITS = [CONV_W, CONV_W, CONV_W, ATT_W, ATT_W, ATT_W, MEM_W, D_MODEL, D_MODEL, D_MODEL]
IN_COLS = sum(IN_SPLITS)
IN_OFFSETS = list(np.cumsum(IN_SPLITS)[:-1])

kernel_name = "hybrid_conv_moba_mem_gated_block"


def rmsnorm(x, g):
    x32 = x.astype(jnp.float32)
    y = x32 * lax.rsqrt(jnp.mean(x32 * x32, axis=-1, keepdims=True) + RMS_EPS)
    return (y * g.astype(jnp.float32)).astype(x.dtype)


def causal_dwconv(u, w, b=None):
    width, c = w.shape
    y = lax.conv_general_dilated(
        u, w.astype(u.dtype)[:, None, :], window_strides=(1,), padding=[(width - 1, 0)],
        dimension_numbers=("NWC", "WIO", "NWC"), feature_group_count=c)
    if b is not None:
        y = y + b.astype(u.dtype)
    return y


def rel_bucket(dist):
    n = jnp.maximum(dist, 0)
    is_small = n < MAX_EXACT
    n_f = jnp.maximum(n, MAX_EXACT).astype(jnp.float32)
    large = MAX_EXACT + (jnp.log(n_f / MAX_EXACT) / math.log(MAX_DISTANCE / MAX_EXACT)
                         * (NUM_BUCKETS - MAX_EXACT)).astype(jnp.int32)
    large = jnp.minimum(large, NUM_BUCKETS - 1)
    return jnp.where(is_small, n, large)


def moba_attention(q, k, v, rel_bias):
    b_, h_, s_, dh = q.shape
    nb = -(-s_ // MOBA_BLOCK)
    pad = nb * MOBA_BLOCK - s_
    k_pad = jnp.pad(k, ((0, 0), (0, 0), (0, pad), (0, 0)))
    v_pad = jnp.pad(v, ((0, 0), (0, 0), (0, pad), (0, 0)))
    kb = k_pad.reshape(b_, h_, nb, MOBA_BLOCK, dh)
    vb = v_pad.reshape(b_, h_, nb, MOBA_BLOCK, dh)
    scale = dh ** -0.5

    k_mean = jnp.mean(kb.astype(jnp.float32), axis=3)
    gate = jnp.einsum("bhsd,bhnd->bhsn", q.astype(jnp.float32), k_mean)
    q_blk = jnp.arange(s_) // MOBA_BLOCK
    past = jnp.arange(nb)[None, :] < q_blk[:, None]
    gate = jnp.where(past, gate, NEG)
    k_sel = max(1, min(MOBA_TOPK, nb - 1))
    _, sel = lax.top_k(gate, k_sel)

    bias_hb = rel_bias.astype(jnp.float32).T
    b_idx = jnp.arange(b_)[:, None, None, None]
    h_idx = jnp.arange(h_)[None, :, None, None]
    blk_off = jnp.arange(MOBA_BLOCK)

    def chunk(c):
        start = c * Q_CHUNK
        qc = lax.dynamic_slice_in_dim(q, start, Q_CHUNK, axis=2)
        selc = lax.dynamic_slice_in_dim(sel, start, Q_CHUNK, axis=2)
        t = start + jnp.arange(Q_CHUNK)
        valid = jnp.arange(k_sel)[None, :] < (t // MOBA_BLOCK)[:, None]
        kg = kb[b_idx, h_idx, selc]
        vg = vb[b_idx, h_idx, selc]
        s_sel = jnp.einsum("bhqd,bhqjkd->bhqjk", qc, kg).astype(jnp.float32) * scale
        k_pos = selc[..., None] * MOBA_BLOCK + blk_off
        bucket = rel_bucket(t[None, None, :, None, None] - k_pos)
        bias = bias_hb[h_idx[..., None], bucket]
        s_sel = jnp.where(valid[None, None, :, :, None], s_sel + bias, NEG)
        own = start // MOBA_BLOCK
        ko = lax.dynamic_slice_in_dim(k_pad, own * MOBA_BLOCK, MOBA_BLOCK, axis=2)
        vo = lax.dynamic_slice_in_dim(v_pad, own * MOBA_BLOCK, MOBA_BLOCK, axis=2)
        s_own = jnp.einsum("bhqd,bhkd->bhqk", qc, ko).astype(jnp.float32) * scale
        dist = t[:, None] - (own * MOBA_BLOCK + blk_off)[None, :]
        bias_own = bias_hb[:, rel_bucket(dist)]
        s_own = jnp.where(dist[None, None] >= 0, s_own + bias_own[None], NEG)
        logits = jnp.concatenate([s_sel.reshape(b_, h_, Q_CHUNK, k_sel * MOBA_BLOCK), s_own], axis=-1)
        p = jax.nn.softmax(logits, axis=-1).astype(v.dtype)
        p_sel = p[..., : k_sel * MOBA_BLOCK].reshape(b_, h_, Q_CHUNK, k_sel, MOBA_BLOCK)
        p_own = p[..., k_sel * MOBA_BLOCK:]
        return (jnp.einsum("bhqjk,bhqjkd->bhqd", p_sel, vg)
                + jnp.einsum("bhqk,bhkd->bhqd", p_own, vo))

    out = lax.map(chunk, jnp.arange(s_ // Q_CHUNK))
    return jnp.transpose(out, (1, 2, 0, 3, 4)).reshape(b_, h_, s_, dh)


def memory_attention(qm, mem_n, w_mem_kv):
    b_, s_, _ = qm.shape
    kv = mem_n @ w_mem_kv
    mk, mv = jnp.split(kv, 2, axis=-1)
    mk = mk.reshape(b_, -1, MEM_HEADS, MEM_DH)
    mv = mv.reshape(b_, -1, MEM_HEADS, MEM_DH)
    q4 = qm.reshape(b_, s_, MEM_HEADS, MEM_DH)
    s = jnp.einsum("bshd,bmhd->bhsm", q4, mk).astype(jnp.float32) * (MEM_DH ** -0.5)
    p = jax.nn.softmax(s, axis=-1).astype(qm.dtype)
    return jnp.einsum("bhsm,bmhd->bshd", p, mv).reshape(b_, s_, MEM_W)


def setup_inputs(seed: int = 0) -> dict:
    key = jax.random.key(seed)
    ks = jax.random.split(key, 20)

    def nrm(k, shape, scale):
        return jax.random.normal(k, shape, jnp.float32) * scale

    def gain(k):
        return 1.0 + nrm(k, (DEPTH, D_MODEL), 0.05)

    return {
        "x": nrm(ks[0], (BATCH, SEQ, D_MODEL), 1.0),
        "mem": nrm(ks[1], (BATCH, MEM_LEN, D_MODEL), 1.0),
        "rel_bias": nrm(ks[2], (NUM_BUCKETS, ATT_HEADS), 0.5),
        "g_pre_mix": gain(ks[3]),
        "g_post_mix": gain(ks[4]),
        "g_pre_ffn": gain(ks[5]),
        "g_post_ffn": gain(ks[6]),
        "g_mem": gain(ks[7]),
        "w_in": nrm(ks[8], (DEPTH, D_MODEL, IN_COLS), D_MODEL ** -0.5),
        "b_gate": nrm(ks[9], (DEPTH, N_BRANCHES * D_MODEL), 0.01),
        "conv_mix_w": nrm(ks[10], (DEPTH, CONV_WIDTH, CONV_W), CONV_WIDTH ** -0.5),
        "w_conv_out": nrm(ks[11], (DEPTH, CONV_W, D_MODEL), CONV_W ** -0.5),
        "w_attn_out": nrm(ks[12], (DEPTH, ATT_W, D_MODEL), ATT_W ** -0.5),
        "w_mem_kv": nrm(ks[13], (DEPTH, D_MODEL, 2 * MEM_W), D_MODEL ** -0.5),
        "w_mem_out": nrm(ks[14], (DEPTH, MEM_W, D_MODEL), MEM_W ** -0.5),
        "w_out": nrm(ks[15], (DEPTH, D_MODEL, D_MODEL), D_MODEL ** -0.5),
        "w_up": nrm(ks[16], (DEPTH, D_MODEL, 2 * D_FF), D_MODEL ** -0.5),
        "ffn_conv_w": nrm(ks[17], (DEPTH, FFN_CONV_WIDTH, 2 * D_FF), FFN_CONV_WIDTH ** -0.5),
        "ffn_conv_b": nrm(ks[18], (DEPTH, 2 * D_FF), 0.01),
        "w_down": nrm(ks[19], (DEPTH, D_FF, D_MODEL), D_FF ** -0.5),
    }


def reference(x, mem, rel_bias, g_pre_mix, g_post_mix, g_pre_ffn, g_post_ffn, g_mem,
              w_in, b_gate, conv_mix_w, w_conv_out, w_attn_out, w_mem_kv, w_mem_out,
              w_out, w_up, ffn_conv_w, ffn_conv_b, w_down):
    b_, s_, d_ = x.shape
    for l in range(DEPTH):
        h = rmsnorm(x, g_pre_mix[l])
        proj = h @ w_in[l]
        (cb, cc, cv, q, k, v, qm, ga, gb, gc) = jnp.split(proj, IN_OFFSETS, axis=-1)
        y_a = (cb * causal_dwconv(cc * cv, conv_mix_w[l])) @ w_conv_out[l]
        to_heads = lambda t: jnp.transpose(t.reshape(b_, s_, ATT_HEADS, ATT_DH), (0, 2, 1, 3))
        o = moba_attention(to_heads(q), to_heads(k), to_heads(v), rel_bias)
        y_b = jnp.transpose(o, (0, 2, 1, 3)).reshape(b_, s_, ATT_W) @ w_attn_out[l]
        y_c = memory_attention(qm, rmsnorm(mem, g_mem[l]), w_mem_kv[l]) @ w_mem_out[l]
        bg = b_gate[l].reshape(N_BRANCHES, d_)
        merged = (jax.nn.sigmoid(ga + bg[0]) * y_a + jax.nn.sigmoid(gb + bg[1]) * y_b
                  + jax.nn.sigmoid(gc + bg[2]) * y_c)
        x = x + rmsnorm(merged @ w_out[l], g_post_mix[l])
        h = rmsnorm(x, g_pre_ffn[l])
        u = causal_dwconv(h @ w_up[l], ffn_conv_w[l], ffn_conv_b[l])
        a, g = jnp.split(u, 2, axis=-1)
        f = (jax.nn.gelu(a) * g) @ w_down[l]
        x = x + rmsnorm(f, g_post_ffn[l])
    return x
```

```python
import functools
import math

import numpy as np
import jax
import jax.numpy as jnp
from jax import lax
from jax.experimental import pallas as pl
from jax.experimental.pallas import tpu as pltpu

F32 = jnp.float32
BF16 = jnp.bfloat16

D_MODEL = 1024
BATCH = 8
SEQ = 2048
DEPTH = 4
TOKENS = BATCH * SEQ
MEM_LEN = 256
CONV_W = 512
ATT_HEADS = 8
ATT_DH = 64
ATT_W = ATT_HEADS * ATT_DH
MOBA_BLOCK = 256
MOBA_TOPK = 3
N_BLOCKS = SEQ // MOBA_BLOCK
MEM_HEADS = 4
MEM_DH = 128
MEM_W = MEM_HEADS * MEM_DH
NUM_BUCKETS = 32
MAX_EXACT = NUM_BUCKETS // 2
MAX_DISTANCE = 1024
D_FF = 2816
RMS_EPS = 1e-6
NEG = -1e30

GATE_W = 3 * D_MODEL
IN_COLS = GATE_W + 3 * CONV_W + 3 * ATT_W + MEM_W
COL_CB = GATE_W
COL_CC = COL_CB + CONV_W
COL_CV = COL_CC + CONV_W
COL_Q = COL_CV + CONV_W
COL_K = COL_Q + ATT_W
COL_V = COL_K + ATT_W
COL_QM = COL_V + ATT_W

LANES = 128
HALO = 16
HEADS_PER_STEP = LANES // ATT_DH
VMEM_LIMIT = 56 * 1024 * 1024

TM_INPROJ = 512
TM_MERGE = 512
TM_FFN = 512
FF_CHUNK = 256


def _bucket_lower_bounds():
    n = np.arange(SEQ)
    n_f = np.maximum(n, MAX_EXACT).astype(np.float64)
    large = MAX_EXACT + (np.log(n_f / MAX_EXACT) / math.log(MAX_DISTANCE / MAX_EXACT)
                         * (NUM_BUCKETS - MAX_EXACT)).astype(np.int32)
    large = np.minimum(large, NUM_BUCKETS - 1)
    bucket = np.where(n < MAX_EXACT, n, large)
    assert np.all(np.diff(bucket) >= 0)
    return tuple(int(np.argmax(bucket >= b)) if np.any(bucket >= b) else SEQ
                 for b in range(NUM_BUCKETS))


def _rms(x, g):
    ms = jnp.mean(x * x, axis=-1, keepdims=True)
    return x * lax.rsqrt(ms + RMS_EPS) * g


def _causal_conv3(ext, w, rows):
    return (w[0:1] * pltpu.roll(ext, 2, 0)[HALO:HALO + rows]
            + w[1:2] * pltpu.roll(ext, 1, 0)[HALO:HALO + rows]
            + w[2:3] * ext[HALO:HALO + rows])


def _bias_kernel(rb_ref, o_ref, *, lows):
    h = pl.program_id(0)
    d = pl.program_id(1)
    row = lax.broadcasted_iota(jnp.int32, (MOBA_BLOCK, MOBA_BLOCK), 0)
    col = lax.broadcasted_iota(jnp.int32, (MOBA_BLOCK, MOBA_BLOCK), 1)
    dist = d * MOBA_BLOCK + row - col
    val = jnp.full((MOBA_BLOCK, MOBA_BLOCK), rb_ref[0, h], F32)
    for b in range(1, NUM_BUCKETS):
        val = jnp.where(dist >= lows[b], rb_ref[b, h], val)
    o_ref[...] = val


def _bias_tiles(rel_bias):
    return pl.pallas_call(
        functools.partial(_bias_kernel, lows=_bucket_lower_bounds()),
        out_shape=jax.ShapeDtypeStruct((ATT_HEADS, N_BLOCKS, MOBA_BLOCK, MOBA_BLOCK), F32),
        grid=(ATT_HEADS, N_BLOCKS),
        in_specs=[pl.BlockSpec(memory_space=pltpu.SMEM)],
        out_specs=pl.BlockSpec((None, None, MOBA_BLOCK, MOBA_BLOCK), lambda h, d: (h, d, 0, 0)),
        name="bias_tiles",
    )(rel_bias)


def _memkv_kernel(mem_ref, g_ref, w_ref, o_ref):
    hn = _rms(mem_ref[...], g_ref[...]).astype(BF16)
    o_ref[...] = jnp.dot(hn, w_ref[...], preferred_element_type=F32).astype(BF16)


def _mem_kv(mem, g_mem, w_mem_kv):
    return pl.pallas_call(
        _memkv_kernel,
        out_shape=jax.ShapeDtypeStruct((DEPTH, BATCH, MEM_LEN, 2 * MEM_W), BF16),
        grid=(DEPTH, BATCH),
        in_specs=[
            pl.BlockSpec((None, MEM_LEN, D_MODEL), lambda l, b: (b, 0, 0)),
            pl.BlockSpec((None, 1, D_MODEL), lambda l, b: (l, 0, 0)),
            pl.BlockSpec((None, D_MODEL, 2 * MEM_W), lambda l, b: (l, 0, 0)),
        ],
        out_specs=pl.BlockSpec((None, None, MEM_LEN, 2 * MEM_W), lambda l, b: (l, b, 0, 0)),
        name="mem_kv",
    )(mem, g_mem, w_mem_kv)


def _inproj_kernel(x_ref, g_ref, w_ref, o_ref):
    h = _rms(x_ref[...], g_ref[...]).astype(BF16)
    for n in range(IN_COLS // CONV_W):
        sl = slice(n * CONV_W, (n + 1) * CONV_W)
        o_ref[:, sl] = jnp.dot(h, w_ref[:, sl], preferred_element_type=F32).astype(BF16)


def _in_proj(x, g_pre, w_in, l):
    tm = TM_INPROJ
    return pl.pallas_call(
        _inproj_kernel,
        out_shape=jax.ShapeDtypeStruct((TOKENS, IN_COLS), BF16),
        grid=(TOKENS // tm,),
        in_specs=[
            pl.BlockSpec((tm, D_MODEL), lambda i: (i, 0)),
            pl.BlockSpec((None, 1, D_MODEL), lambda i: (l, 0, 0)),
            pl.BlockSpec((None, D_MODEL, IN_COLS), lambda i: (l, 0, 0),
                         pipeline_mode=pl.Buffered(1)),
        ],
        out_specs=pl.BlockSpec((tm, IN_COLS), lambda i: (i, 0)),
        compiler_params=pltpu.CompilerParams(
            dimension_semantics=("arbitrary",), vmem_limit_bytes=VMEM_LIMIT),
        name="in_proj",
    )(x, g_pre, w_in)


def _moba_kernel(q_ref, k_ref, v_ref, bias_ref, o_ref,
                 vm_ref, km_ref, s_ref, mx_ref, l_ref, acc_ref, o2_ref):
    blk = MOBA_BLOCK
    lane_row = lax.broadcasted_iota(jnp.int32, (1, LANES), 1)
    head_masks = [(lane_row // ATT_DH) == h for h in range(HEADS_PER_STEP)]

    for h in range(HEADS_PER_STEP):
        vm_ref[h] = jnp.where(head_masks[h], v_ref[...], jnp.zeros((), BF16))
        km_ref[h] = jnp.zeros((LANES, LANES), F32)
        for j in range(N_BLOCKS):
            kj = jnp.where(head_masks[h], k_ref[j * blk:(j + 1) * blk, :].astype(F32), 0.0)
            km_ref[h, j:j + 1, :] = jnp.sum(kj, axis=0, keepdims=True) / blk

    row = lax.broadcasted_iota(jnp.int32, (blk, blk), 0)
    col = lax.broadcasted_iota(jnp.int32, (blk, blk), 1)
    causal = row >= col
    lane_q = lax.broadcasted_iota(jnp.int32, (blk, LANES), 1)
    nt = (((1,), (1,)), ((), ()))

    def q_block(i, carry):
        q_rows = pl.ds(pl.multiple_of(i * blk, blk), blk)
        q_i = q_ref[q_rows, :]
        for h in range(HEADS_PER_STEP):
            qm = jnp.where(head_masks[h], q_i, jnp.zeros((), BF16)) * (ATT_DH ** -0.5)
            qm = qm.astype(BF16)

            gate = lax.dot_general(qm.astype(F32), km_ref[h], nt,
                                   precision=lax.Precision.HIGHEST,
                                   preferred_element_type=F32)
            past = lane_q < i
            selected = []
            for j in range(N_BLOCKS - 1):
                gj = gate[:, j:j + 1]
                beats = (gate > gj) | ((gate == gj) & (lane_q < j))
                rank = jnp.sum(jnp.where(beats & past, 1.0, 0.0), axis=1, keepdims=True)
                selected.append(rank < float(MOBA_TOPK))

            k_own = k_ref[q_rows, :]
            s_own = lax.dot_general(qm, k_own, nt, preferred_element_type=F32) + bias_ref[h, 0]
            s_own = jnp.where(causal, s_own, NEG)
            s_ref[:, (N_BLOCKS - 1) * blk:] = s_own
            mx_ref[...] = s_own
            for j in range(N_BLOCKS - 1):
                @pl.when(j < i)
                def _():
                    k_j = k_ref[j * blk:(j + 1) * blk, :]
                    s = lax.dot_general(qm, k_j, nt, preferred_element_type=F32) + bias_ref[h, i - j]
                    s = jnp.where(selected[j], s, NEG)
                    s_ref[:, j * blk:(j + 1) * blk] = s
                    mx_ref[...] = jnp.maximum(mx_ref[...], s)
            m = jnp.max(mx_ref[...], axis=1, keepdims=True)

            p = jnp.exp(s_ref[:, (N_BLOCKS - 1) * blk:] - m)
            l_ref[...] = p
            acc_ref[...] = jnp.dot(p.astype(BF16), vm_ref[h, q_rows, :], preferred_element_type=F32)
            for j in range(N_BLOCKS - 1):
                @pl.when(j < i)
                def _():
                    pj = jnp.exp(s_ref[:, j * blk:(j + 1) * blk] - m)
                    l_ref[...] += pj
                    acc_ref[...] += jnp.dot(pj.astype(BF16), vm_ref[h, j * blk:(j + 1) * blk, :],
                                            preferred_element_type=F32)
            out_h = acc_ref[...] / jnp.sum(l_ref[...], axis=1, keepdims=True)
            if h == 0:
                o2_ref[...] = out_h
            else:
                o2_ref[...] += out_h
        o_ref[q_rows, :] = o2_ref[...].astype(BF16)
        return carry

    lax.fori_loop(0, N_BLOCKS, q_block, 0)


def _moba(proj, bias):
    pairs = ATT_HEADS // HEADS_PER_STEP
    blk = MOBA_BLOCK
    return pl.pallas_call(
        _moba_kernel,
        out_shape=jax.ShapeDtypeStruct((TOKENS, ATT_W), BF16),
        grid=(BATCH, pairs),
        in_specs=[
            pl.BlockSpec((SEQ, LANES), lambda b, p: (b, COL_Q // LANES + p)),
            pl.BlockSpec((SEQ, LANES), lambda b, p: (b, COL_K // LANES + p)),
            pl.BlockSpec((SEQ, LANES), lambda b, p: (b, COL_V // LANES + p)),
            pl.BlockSpec((HEADS_PER_STEP, N_BLOCKS, blk, blk), lambda b, p: (p, 0, 0, 0)),
        ],
        out_specs=pl.BlockSpec((SEQ, LANES), lambda b, p: (b, p)),
        scratch_shapes=[
            pltpu.VMEM((HEADS_PER_STEP, SEQ, LANES), BF16),
            pltpu.VMEM((HEADS_PER_STEP, LANES, LANES), F32),
            pltpu.VMEM((blk, SEQ), F32),
            pltpu.VMEM((blk, blk), F32),
            pltpu.VMEM((blk, blk), F32),
            pltpu.VMEM((blk, LANES), F32),
            pltpu.VMEM((blk, LANES), F32),
        ],
        compiler_params=pltpu.CompilerParams(
            dimension_semantics=("arbitrary", "arbitrary"), vmem_limit_bytes=VMEM_LIMIT),
        name="moba",
    )(proj, proj, proj, bias)


def _merge_kernel(x_ref, gates_ref, cb_ref, cc_ref, cv_ref, cch_ref, cvh_ref, qm_ref, o_ref, kv_ref,
                  bg_ref, cw_ref, wa_ref, wb_ref, wc_ref, wo_ref, gp_ref, out_ref, *, tiles_per_seq):
    i = pl.program_id(0)
    tm = x_ref.shape[0]

    ccv = cc_ref[...].astype(F32) * cv_ref[...].astype(F32)
    halo = cch_ref[...].astype(F32) * cvh_ref[...].astype(F32)
    halo = jnp.where(i % tiles_per_seq == 0, 0.0, halo)
    conv = _causal_conv3(jnp.concatenate([halo, ccv], axis=0), cw_ref[...], tm)
    ya = jnp.dot((cb_ref[...].astype(F32) * conv).astype(BF16), wa_ref[...], preferred_element_type=F32)

    yb = jnp.dot(o_ref[...], wb_ref[...], preferred_element_type=F32)

    nt = (((1,), (1,)), ((), ()))
    heads = []
    for h in range(MEM_HEADS):
        sl = slice(h * MEM_DH, (h + 1) * MEM_DH)
        s = lax.dot_general(qm_ref[:, sl], kv_ref[:, sl], nt, preferred_element_type=F32) * (MEM_DH ** -0.5)
        m = jnp.max(s, axis=1, keepdims=True)
        p = jnp.exp(s - m)
        pv = jnp.dot(p.astype(BF16), kv_ref[:, MEM_W + h * MEM_DH:MEM_W + (h + 1) * MEM_DH],
                     preferred_element_type=F32)
        heads.append(pv / jnp.sum(p, axis=1, keepdims=True))
    om = jnp.concatenate(heads, axis=1).astype(BF16)
    yc = jnp.dot(om, wc_ref[...], preferred_element_type=F32)

    gates = jax.nn.sigmoid(gates_ref[...].astype(F32) + bg_ref[...])
    merged = (gates[:, 0:D_MODEL] * ya + gates[:, D_MODEL:2 * D_MODEL] * yb
              + gates[:, 2 * D_MODEL:3 * D_MODEL] * yc)
    z = jnp.dot(merged.astype(BF16), wo_ref[...], preferred_element_type=F32)
    out_ref[...] = x_ref[...] + _rms(z, gp_ref[...])


def _merge(x, proj, o, kv, b_gate, conv_w, w_a, w_b, w_c, w_o, g_post, l):
    tm = TM_MERGE
    tiles_per_seq = SEQ // tm
    halo_blocks = tm // HALO
    c512 = lambda col: col // CONV_W
    weight = lambda rows: pl.BlockSpec((None, rows, D_MODEL), lambda i: (l, 0, 0),
                                       pipeline_mode=pl.Buffered(1))
    halo_map = lambda col: (lambda i: (jnp.maximum(i * halo_blocks - 1, 0), c512(col)))
    return pl.pallas_call(
        functools.partial(_merge_kernel, tiles_per_seq=tiles_per_seq),
        out_shape=jax.ShapeDtypeStruct((TOKENS, D_MODEL), F32),
        grid=(TOKENS // tm,),
        in_specs=[
            pl.BlockSpec((tm, D_MODEL), lambda i: (i, 0)),
            pl.BlockSpec((tm, GATE_W), lambda i: (i, 0)),
            pl.BlockSpec((tm, CONV_W), lambda i: (i, c512(COL_CB))),
            pl.BlockSpec((tm, CONV_W), lambda i: (i, c512(COL_CC))),
            pl.BlockSpec((tm, CONV_W), lambda i: (i, c512(COL_CV))),
            pl.BlockSpec((HALO, CONV_W), halo_map(COL_CC)),
            pl.BlockSpec((HALO, CONV_W), halo_map(COL_CV)),
            pl.BlockSpec((tm, MEM_W), lambda i: (i, c512(COL_QM))),
            pl.BlockSpec((tm, ATT_W), lambda i: (i, 0)),
            pl.BlockSpec((None, None, MEM_LEN, 2 * MEM_W), lambda i: (l, i // tiles_per_seq, 0, 0)),
            pl.BlockSpec((None, 1, GATE_W), lambda i: (l, 0, 0)),
            pl.BlockSpec((None, 3, CONV_W), lambda i: (l, 0, 0)),
            weight(CONV_W), weight(ATT_W), weight(MEM_W), weight(D_MODEL),
            pl.BlockSpec((None, 1, D_MODEL), lambda i: (l, 0, 0)),
        ],
        out_specs=pl.BlockSpec((tm, D_MODEL), lambda i: (i, 0)),
        compiler_params=pltpu.CompilerParams(
            dimension_semantics=("arbitrary",), vmem_limit_bytes=VMEM_LIMIT),
        name="merge",
    )(x, proj, proj, proj, proj, proj, proj, proj, o, kv, b_gate, conv_w, w_a, w_b, w_c, w_o, g_post)


def _gelu_tanh(x):
    cdf = 0.5 * (1.0 + jnp.tanh(math.sqrt(2.0 / math.pi) * (x + 0.044715 * (x * x * x))))
    return x * cdf


def _ffn_kernel(x_ref, xh_ref, gpre_ref, wup_ref, cw_ref, cb_ref, wdn_ref, gpost_ref, out_ref,
                act_ref, *, tiles_per_seq):
    i = pl.program_id(0)
    tm = x_ref.shape[0]
    x = x_ref[...]
    h = _rms(x, gpre_ref[...])
    hh = _rms(xh_ref[...], gpre_ref[...])
    hh = jnp.where(i % tiles_per_seq == 0, 0.0, hh)
    hx = jnp.concatenate([hh.astype(BF16), h.astype(BF16)], axis=0)

    for f in range(D_FF // FF_CHUNK):
        halves = []
        for base in (0, D_FF):
            sl = slice(base + f * FF_CHUNK, base + (f + 1) * FF_CHUNK)
            u = jnp.dot(hx, wup_ref[:, sl], preferred_element_type=F32)
            halves.append(_causal_conv3(u, cw_ref[:, sl], tm) + cb_ref[:, sl])
        act_ref[:, f * FF_CHUNK:(f + 1) * FF_CHUNK] = (_gelu_tanh(halves[0]) * halves[1]).astype(BF16)

    y = jnp.dot(act_ref[...], wdn_ref[...], preferred_element_type=F32)
    out_ref[...] = x + _rms(y, gpost_ref[...])


def _ffn(x, g_pre, w_up, conv_w, conv_b, w_down, g_post, l):
    tm = TM_FFN
    tiles_per_seq = SEQ // tm
    halo_blocks = tm // HALO
    vec = lambda rows, cols: pl.BlockSpec((None, rows, cols), lambda i: (l, 0, 0))
    return pl.pallas_call(
        functools.partial(_ffn_kernel, tiles_per_seq=tiles_per_seq),
        out_shape=jax.ShapeDtypeStruct((TOKENS, D_MODEL), F32),
        grid=(TOKENS // tm,),
        in_specs=[
            pl.BlockSpec((tm, D_MODEL), lambda i: (i, 0)),
            pl.BlockSpec((HALO, D_MODEL), lambda i: (jnp.maximum(i * halo_blocks - 1, 0), 0)),
            vec(1, D_MODEL),
            pl.BlockSpec((None, D_MODEL, 2 * D_FF), lambda i: (l, 0, 0), pipeline_mode=pl.Buffered(1)),
            vec(3, 2 * D_FF),
            vec(1, 2 * D_FF),
            pl.BlockSpec((None, D_FF, D_MODEL), lambda i: (l, 0, 0), pipeline_mode=pl.Buffered(1)),
            vec(1, D_MODEL),
        ],
        out_specs=pl.BlockSpec((tm, D_MODEL), lambda i: (i, 0)),
        scratch_shapes=[pltpu.VMEM((tm, D_FF), BF16)],
        compiler_params=pltpu.CompilerParams(
            dimension_semantics=("arbitrary",), vmem_limit_bytes=VMEM_LIMIT),
        name="conv_ffn",
    )(x, x, g_pre, w_up, conv_w, conv_b, w_down, g_post)


def kernel(x, mem, rel_bias, g_pre_mix, g_post_mix, g_pre_ffn, g_post_ffn, g_mem, w_in, b_gate,
           conv_mix_w, w_conv_out, w_attn_out, w_mem_kv, w_mem_out, w_out, w_up, ffn_conv_w,
           ffn_conv_b, w_down):
    assert x.shape == (BATCH, SEQ, D_MODEL) and mem.shape == (BATCH, MEM_LEN, D_MODEL)
    assert w_in.shape == (DEPTH, D_MODEL, IN_COLS)

    row3 = lambda a: a.reshape(DEPTH, 1, a.shape[-1])
    gate_col = 3 * CONV_W + 3 * ATT_W + MEM_W
    w_in_b = jnp.concatenate([w_in[:, :, gate_col:], w_in[:, :, :gate_col]], axis=2).astype(BF16)
    w_a, w_b, w_c, w_o = (w.astype(BF16) for w in (w_conv_out, w_attn_out, w_mem_out, w_out))
    w_up_b, w_dn_b, w_kv_b = w_up.astype(BF16), w_down.astype(BF16), w_mem_kv.astype(BF16)

    bias = _bias_tiles(rel_bias)
    kv = _mem_kv(mem, row3(g_mem), w_kv_b)

    xt = x.reshape(TOKENS, D_MODEL)
    for l in range(DEPTH):
        proj = _in_proj(xt, row3(g_pre_mix), w_in_b, l)
        o = _moba(proj, bias)
        xt = _merge(xt, proj, o, kv, row3(b_gate), conv_mix_w, w_a, w_b, w_c, w_o, row3(g_post_mix), l)
        xt = _ffn(xt, row3(g_pre_ffn), w_up_b, ffn_conv_w, row3(ffn_conv_b), w_dn_b, row3(g_post_ffn), l)
    return xt.reshape(BATCH, SEQ, D_MODEL)
```

```python
import functools
import math

import numpy as np
import jax
import jax.numpy as jnp
from jax import lax
from jax.experimental import pallas as pl
from jax.experimental.pallas import tpu as pltpu

F32 = jnp.float32
BF16 = jnp.bfloat16

D_MODEL = 1024
BATCH = 8
SEQ = 2048
DEPTH = 4
TOKENS = BATCH * SEQ
MEM_LEN = 256
CONV_W = 512
ATT_HEADS = 8
ATT_DH = 64
ATT_W = ATT_HEADS * ATT_DH
MOBA_BLOCK = 256
MOBA_TOPK = 3
N_BLOCKS = SEQ // MOBA_BLOCK
MEM_HEADS = 4
MEM_DH = 128
MEM_W = MEM_HEADS * MEM_DH
NUM_BUCKETS = 32
MAX_EXACT = NUM_BUCKETS // 2
MAX_DISTANCE = 1024
D_FF = 2816
RMS_EPS = 1e-6
NEG = -1e30

GATE_W = 3 * D_MODEL
IN_COLS = GATE_W + 3 * CONV_W + 3 * ATT_W + MEM_W
PROJ_COLS = IN_COLS - ATT_W
COL_CB = GATE_W
COL_CC = COL_CB + CONV_W
COL_CV = COL_CC + CONV_W
COL_Q = COL_CV + CONV_W
COL_K = COL_Q + ATT_W
COL_QM = COL_K + ATT_W

LANES = 128
HALO = 16
HEADS_PER_STEP = LANES // ATT_DH
VMEM_LIMIT = 56 * 1024 * 1024

TM_INPROJ = 512
TM_MERGE = 512
TM_FFN = 512
FF_CHUNK = 256


def _bucket_lower_bounds():
    n = np.arange(SEQ)
    n_f = np.maximum(n, MAX_EXACT).astype(np.float64)
    large = MAX_EXACT + (np.log(n_f / MAX_EXACT) / math.log(MAX_DISTANCE / MAX_EXACT)
                         * (NUM_BUCKETS - MAX_EXACT)).astype(np.int32)
    large = np.minimum(large, NUM_BUCKETS - 1)
    bucket = np.where(n < MAX_EXACT, n, large)
    assert np.all(np.diff(bucket) >= 0)
    return tuple(int(np.argmax(bucket >= b)) if np.any(bucket >= b) else SEQ
                 for b in range(NUM_BUCKETS))


def _rms(x, g):
    ms = jnp.mean(x * x, axis=-1, keepdims=True)
    return x * lax.rsqrt(ms + RMS_EPS) * g


def _causal_conv3(ext, w, rows):
    return (w[0:1] * pltpu.roll(ext, 2, 0)[HALO:HALO + rows]
            + w[1:2] * pltpu.roll(ext, 1, 0)[HALO:HALO + rows]
            + w[2:3] * ext[HALO:HALO + rows])


def _bias_kernel(rb_ref, o_ref, *, lows):
    h = pl.program_id(0)
    c = pl.program_id(1)
    key = lax.broadcasted_iota(jnp.int32, (MOBA_BLOCK, MOBA_BLOCK), 0)
    qry = lax.broadcasted_iota(jnp.int32, (MOBA_BLOCK, MOBA_BLOCK), 1)
    dist = (N_BLOCKS - 1 - c) * MOBA_BLOCK + qry - key
    val = jnp.full((MOBA_BLOCK, MOBA_BLOCK), rb_ref[0, h], F32)
    for b in range(1, NUM_BUCKETS):
        val = jnp.where(dist >= lows[b], rb_ref[b, h], val)
    o_ref[...] = val


def _bias_strips(rel_bias):
    return pl.pallas_call(
        functools.partial(_bias_kernel, lows=_bucket_lower_bounds()),
        out_shape=jax.ShapeDtypeStruct((ATT_HEADS, SEQ, MOBA_BLOCK), F32),
        grid=(ATT_HEADS, N_BLOCKS),
        in_specs=[pl.BlockSpec(memory_space=pltpu.SMEM)],
        out_specs=pl.BlockSpec((None, MOBA_BLOCK, MOBA_BLOCK), lambda h, c: (h, c, 0)),
        name="bias_strips",
    )(rel_bias)


def _memkv_kernel(mem_ref, g_ref, w_ref, o_ref):
    hn = _rms(mem_ref[...], g_ref[...]).astype(BF16)
    o_ref[...] = jnp.dot(hn, w_ref[...], preferred_element_type=F32).astype(BF16)


def _mem_kv(mem, g_mem, w_mem_kv):
    return pl.pallas_call(
        _memkv_kernel,
        out_shape=jax.ShapeDtypeStruct((DEPTH, BATCH, MEM_LEN, 2 * MEM_W), BF16),
        grid=(DEPTH, BATCH),
        in_specs=[
            pl.BlockSpec((None, MEM_LEN, D_MODEL), lambda l, b: (b, 0, 0)),
            pl.BlockSpec((None, 1, D_MODEL), lambda l, b: (l, 0, 0)),
            pl.BlockSpec((None, D_MODEL, 2 * MEM_W), lambda l, b: (l, 0, 0)),
        ],
        out_specs=pl.BlockSpec((None, None, MEM_LEN, 2 * MEM_W), lambda l, b: (l, b, 0, 0)),
        name="mem_kv",
    )(mem, g_mem, w_mem_kv)


def _inproj_kernel(x_ref, g_ref, w_ref, wvt_ref, o_ref, vt_ref):
    h = _rms(x_ref[...], g_ref[...]).astype(BF16)
    for n in range(PROJ_COLS // CONV_W):
        sl = slice(n * CONV_W, (n + 1) * CONV_W)
        o_ref[:, sl] = jnp.dot(h, w_ref[:, sl], preferred_element_type=F32).astype(BF16)
    vt_ref[...] = lax.dot_general(wvt_ref[...], h, (((1,), (1,)), ((), ())),
                                  preferred_element_type=F32).astype(BF16)


def _in_proj(x, g_pre, w_in, w_vt, l):
    tm = TM_INPROJ
    return pl.pallas_call(
        _inproj_kernel,
        out_shape=(jax.ShapeDtypeStruct((TOKENS, PROJ_COLS), BF16),
                   jax.ShapeDtypeStruct((ATT_W, TOKENS), BF16)),
        grid=(TOKENS // tm,),
        in_specs=[
            pl.BlockSpec((tm, D_MODEL), lambda i: (i, 0)),
            pl.BlockSpec((None, 1, D_MODEL), lambda i: (l, 0, 0)),
            pl.BlockSpec((None, D_MODEL, PROJ_COLS), lambda i: (l, 0, 0),
                         pipeline_mode=pl.Buffered(1)),
            pl.BlockSpec((None, ATT_W, D_MODEL), lambda i: (l, 0, 0),
                         pipeline_mode=pl.Buffered(1)),
        ],
        out_specs=(pl.BlockSpec((tm, PROJ_COLS), lambda i: (i, 0)),
                   pl.BlockSpec((ATT_W, tm), lambda i: (0, i))),
        compiler_params=pltpu.CompilerParams(
            dimension_semantics=("arbitrary",), vmem_limit_bytes=VMEM_LIMIT),
        name="in_proj",
    )(x, g_pre, w_in, w_vt)


def _split3_bf16(a):
    hi = a.astype(BF16)
    r1 = a - hi.astype(F32)
    mid = r1.astype(BF16)
    lo = (r1 - mid.astype(F32)).astype(BF16)
    return hi, mid, lo


def _moba_kernel(q_ref, k_ref, vt_ref, bias_ref, o_ref):
    blk = MOBA_BLOCK
    pad = 2 * N_BLOCKS
    nt = (((1,), (1,)), ((), ()))
    lane_row = lax.broadcasted_iota(jnp.int32, (1, LANES), 1)
    head_masks = [(lane_row // ATT_DH) == h for h in range(HEADS_PER_STEP)]
    key = lax.broadcasted_iota(jnp.int32, (blk, blk), 0)
    qry = lax.broadcasted_iota(jnp.int32, (blk, blk), 1)
    causal = key <= qry

    sums = [jnp.sum(k_ref[j * blk:(j + 1) * blk, :].astype(F32), axis=0, keepdims=True)
            for j in range(N_BLOCKS)]
    kmean = jnp.concatenate(sums + [jnp.zeros((pad - N_BLOCKS, LANES), F32)], axis=0) / blk
    kmean3 = jnp.concatenate(_split3_bf16(kmean), axis=0)

    def logits(i, h):
        q_i = q_ref[i * blk:(i + 1) * blk, :]
        qm = (jnp.where(head_masks[h], q_i, jnp.zeros((), BF16)) * (ATT_DH ** -0.5)).astype(BF16)
        st = lax.dot_general(k_ref[0:(i + 1) * blk, :], qm, nt, preferred_element_type=F32)

        selected = None
        if i > MOBA_TOPK:
            g3 = lax.dot_general(kmean3, qm, nt, preferred_element_type=F32)
            gate = g3[0:N_BLOCKS] + g3[pad:pad + N_BLOCKS] + g3[2 * pad:2 * pad + N_BLOCKS]
            rows = [gate[j:j + 1, :] for j in range(i)]
            rank = [jnp.zeros((1, blk), F32) for _ in range(i)]
            for a in range(i):
                for b in range(a + 1, i):
                    a_wins = rows[a] >= rows[b]
                    rank[b] = rank[b] + jnp.where(a_wins, 1.0, 0.0)
                    rank[a] = rank[a] + jnp.where(a_wins, 0.0, 1.0)
            selected = [r < float(MOBA_TOPK) for r in rank]

        base = (N_BLOCKS - 1 - i) * blk
        pieces = []
        for j in range(i + 1):
            s = st[j * blk:(j + 1) * blk, :] + bias_ref[h, base + j * blk:base + (j + 1) * blk, :]
            if j == i:
                s = jnp.where(causal, s, NEG)
            elif selected is not None:
                s = jnp.where(selected[j], s, NEG)
            pieces.append(s)
        return pieces

    def attend(h, pieces):
        mx = pieces[0]
        for s in pieces[1:]:
            mx = jnp.maximum(mx, s)
        m = jnp.max(mx, axis=0, keepdims=True)
        acc = None
        psum = None
        for j, s in enumerate(pieces):
            p = jnp.exp(s - m)
            psum = p if psum is None else psum + p
            pv = jnp.dot(vt_ref[h * ATT_DH:(h + 1) * ATT_DH, j * blk:(j + 1) * blk], p.astype(BF16),
                         preferred_element_type=F32)
            acc = pv if acc is None else acc + pv
        return acc / jnp.sum(psum, axis=0, keepdims=True)

    items = [(i, h) for i in range(N_BLOCKS) for h in range(HEADS_PER_STEP)]
    outs = []
    pending = logits(*items[0])
    for n, (i, h) in enumerate(items):
        upcoming = logits(*items[n + 1]) if n + 1 < len(items) else None
        outs.append(attend(h, pending))
        pending = upcoming
        if h == HEADS_PER_STEP - 1:
            o_ref[i * blk:(i + 1) * blk, :] = jnp.concatenate(outs, axis=0).T.astype(BF16)
            outs = []


def _moba(proj, vt, bias):
    pairs = ATT_HEADS // HEADS_PER_STEP
    return pl.pallas_call(
        _moba_kernel,
        out_shape=jax.ShapeDtypeStruct((TOKENS, ATT_W), BF16),
        grid=(BATCH, pairs),
        in_specs=[
            pl.BlockSpec((SEQ, LANES), lambda b, p: (b, COL_Q // LANES + p)),
            pl.BlockSpec((SEQ, LANES), lambda b, p: (b, COL_K // LANES + p)),
            pl.BlockSpec((LANES, SEQ), lambda b, p: (p, b)),
            pl.BlockSpec((HEADS_PER_STEP, SEQ, MOBA_BLOCK), lambda b, p: (p, 0, 0)),
        ],
        out_specs=pl.BlockSpec((SEQ, LANES), lambda b, p: (b, p)),
        compiler_params=pltpu.CompilerParams(
            dimension_semantics=("arbitrary", "arbitrary"), vmem_limit_bytes=VMEM_LIMIT),
        name="moba",
    )(proj, proj, vt, bias)


def _merge_kernel(x_ref, gates_ref, cb_ref, cc_ref, cv_ref, cch_ref, cvh_ref, qm_ref, o_ref, kv_ref,
                  bg_ref, cw_ref, wa_ref, wb_ref, wc_ref, wo_ref, gp_ref, out_ref, *, tiles_per_seq):
    i = pl.program_id(0)
    tm = x_ref.shape[0]

    ccv = cc_ref[...].astype(F32) * cv_ref[...].astype(F32)
    halo = cch_ref[...].astype(F32) * cvh_ref[...].astype(F32)
    halo = jnp.where(i % tiles_per_seq == 0, 0.0, halo)
    conv = _causal_conv3(jnp.concatenate([halo, ccv], axis=0), cw_ref[...], tm)
    ya = jnp.dot((cb_ref[...].astype(F32) * conv).astype(BF16), wa_ref[...], preferred_element_type=F32)

    yb = jnp.dot(o_ref[...], wb_ref[...], preferred_element_type=F32)

    nt = (((1,), (1,)), ((), ()))
    heads = []
    for h in range(MEM_HEADS):
        sl = slice(h * MEM_DH, (h + 1) * MEM_DH)
        s = lax.dot_general(qm_ref[:, sl], kv_ref[:, sl], nt, preferred_element_type=F32) * (MEM_DH ** -0.5)
        m = jnp.max(s, axis=1, keepdims=True)
        p = jnp.exp(s - m)
        pv = jnp.dot(p.astype(BF16), kv_ref[:, MEM_W + h * MEM_DH:MEM_W + (h + 1) * MEM_DH],
                     preferred_element_type=F32)
        heads.append(pv / jnp.sum(p, axis=1, keepdims=True))
    om = jnp.concatenate(heads, axis=1).astype(BF16)
    yc = jnp.dot(om, wc_ref[...], preferred_element_type=F32)

    gates = jax.nn.sigmoid(gates_ref[...].astype(F32) + bg_ref[...])
    merged = (gates[:, 0:D_MODEL] * ya + gates[:, D_MODEL:2 * D_MODEL] * yb
              + gates[:, 2 * D_MODEL:3 * D_MODEL] * yc)
    z = jnp.dot(merged.astype(BF16), wo_ref[...], preferred_element_type=F32)
    out_ref[...] = x_ref[...] + _rms(z, gp_ref[...])


def _merge(x, proj, o, kv, b_gate, conv_w, w_a, w_b, w_c, w_o, g_post, l):
    tm = TM_MERGE
    tiles_per_seq = SEQ // tm
    halo_blocks = tm // HALO
    c512 = lambda col: col // CONV_W
    weight = lambda rows: pl.BlockSpec((None, rows, D_MODEL), lambda i: (l, 0, 0),
                                       pipeline_mode=pl.Buffered(1))
    halo_map = lambda col: (lambda i: (jnp.maximum(i * halo_blocks - 1, 0), c512(col)))
    return pl.pallas_call(
        functools.partial(_merge_kernel, tiles_per_seq=tiles_per_seq),
        out_shape=jax.ShapeDtypeStruct((TOKENS, D_MODEL), F32),
        grid=(TOKENS // tm,),
        in_specs=[
            pl.BlockSpec((tm, D_MODEL), lambda i: (i, 0)),
            pl.BlockSpec((tm, GATE_W), lambda i: (i, 0)),
            pl.BlockSpec((tm, CONV_W), lambda i: (i, c512(COL_CB))),
            pl.BlockSpec((tm, CONV_W), lambda i: (i, c512(COL_CC))),
            pl.BlockSpec((tm, CONV_W), lambda i: (i, c512(COL_CV))),
            pl.BlockSpec((HALO, CONV_W), halo_map(COL_CC)),
            pl.BlockSpec((HALO, CONV_W), halo_map(COL_CV)),
            pl.BlockSpec((tm, MEM_W), lambda i: (i, c512(COL_QM))),
            pl.BlockSpec((tm, ATT_W), lambda i: (i, 0)),
            pl.BlockSpec((None, None, MEM_LEN, 2 * MEM_W), lambda i: (l, i // tiles_per_seq, 0, 0)),
            pl.BlockSpec((None, 1, GATE_W), lambda i: (l, 0, 0)),
            pl.BlockSpec((None, 3, CONV_W), lambda i: (l, 0, 0)),
            weight(CONV_W), weight(ATT_W), weight(MEM_W), weight(D_MODEL),
            pl.BlockSpec((None, 1, D_MODEL), lambda i: (l, 0, 0)),
        ],
        out_specs=pl.BlockSpec((tm, D_MODEL), lambda i: (i, 0)),
        compiler_params=pltpu.CompilerParams(
            dimension_semantics=("arbitrary",), vmem_limit_bytes=VMEM_LIMIT),
        name="merge",
    )(x, proj, proj, proj, proj, proj, proj, proj, o, kv, b_gate, conv_w, w_a, w_b, w_c, w_o, g_post)


def _gelu_tanh(x):
    cdf = 0.5 * (1.0 + jnp.tanh(math.sqrt(2.0 / math.pi) * (x + 0.044715 * (x * x * x))))
    return x * cdf


def _ffn_kernel(x_ref, xh_ref, gpre_ref, wup_ref, cw_ref, cb_ref, wdn_ref, gpost_ref, out_ref,
                act_ref, *, tiles_per_seq):
    i = pl.program_id(0)
    tm = x_ref.shape[0]
    x = x_ref[...]
    h = _rms(x, gpre_ref[...])
    hh = _rms(xh_ref[...], gpre_ref[...])
    hh = jnp.where(i % tiles_per_seq == 0, 0.0, hh)
    hx = jnp.concatenate([hh.astype(BF16), h.astype(BF16)], axis=0)

    for f in range(D_FF // FF_CHUNK):
        halves = []
        for base in (0, D_FF):
            sl = slice(base + f * FF_CHUNK, base + (f + 1) * FF_CHUNK)
            u = jnp.dot(hx, wup_ref[:, sl], preferred_element_type=F32)
            halves.append(_causal_conv3(u, cw_ref[:, sl], tm) + cb_ref[:, sl])
        act_ref[:, f * FF_CHUNK:(f + 1) * FF_CHUNK] = (_gelu_tanh(halves[0]) * halves[1]).astype(BF16)

    y = jnp.dot(act_ref[...], wdn_ref[...], preferred_element_type=F32)
    out_ref[...] = x + _rms(y, gpost_ref[...])


def _ffn(x, g_pre, w_up, conv_w, conv_b, w_down, g_post, l):
    tm = TM_FFN
    tiles_per_seq = SEQ // tm
    halo_blocks = tm // HALO
    vec = lambda rows, cols: pl.BlockSpec((None, rows, cols), lambda i: (l, 0, 0))
    return pl.pallas_call(
        functools.partial(_ffn_kernel, tiles_per_seq=tiles_per_seq),
        out_shape=jax.ShapeDtypeStruct((TOKENS, D_MODEL), F32),
        grid=(TOKENS // tm,),
        in_specs=[
            pl.BlockSpec((tm, D_MODEL), lambda i: (i, 0)),
            pl.BlockSpec((HALO, D_MODEL), lambda i: (jnp.maximum(i * halo_blocks - 1, 0), 0)),
            vec(1, D_MODEL),
            pl.BlockSpec((None, D_MODEL, 2 * D_FF), lambda i: (l, 0, 0), pipeline_mode=pl.Buffered(1)),
            vec(3, 2 * D_FF),
            vec(1, 2 * D_FF),
            pl.BlockSpec((None, D_FF, D_MODEL), lambda i: (l, 0, 0), pipeline_mode=pl.Buffered(1)),
            vec(1, D_MODEL),
        ],
        out_specs=pl.BlockSpec((tm, D_MODEL), lambda i: (i, 0)),
        scratch_shapes=[pltpu.VMEM((tm, D_FF), BF16)],
        compiler_params=pltpu.CompilerParams(
            dimension_semantics=("arbitrary",), vmem_limit_bytes=VMEM_LIMIT),
        name="conv_ffn",
    )(x, x, g_pre, w_up, conv_w, conv_b, w_down, g_post)


def kernel(x, mem, rel_bias, g_pre_mix, g_post_mix, g_pre_ffn, g_post_ffn, g_mem, w_in, b_gate,
           conv_mix_w, w_conv_out, w_attn_out, w_mem_kv, w_mem_out, w_out, w_up, ffn_conv_w,
           ffn_conv_b, w_down):
    assert x.shape == (BATCH, SEQ, D_MODEL) and mem.shape == (BATCH, MEM_LEN, D_MODEL)
    assert w_in.shape == (DEPTH, D_MODEL, IN_COLS)

    row3 = lambda a: a.reshape(DEPTH, 1, a.shape[-1])
    col_v = 3 * CONV_W + 2 * ATT_W
    col_qm = col_v + ATT_W
    col_gate = col_qm + MEM_W
    w_in_b = jnp.concatenate([w_in[:, :, col_gate:], w_in[:, :, :col_v], w_in[:, :, col_qm:col_gate]],
                             axis=2).astype(BF16)
    w_vt_b = jnp.swapaxes(w_in[:, :, col_v:col_qm], 1, 2).astype(BF16)
    w_a, w_b, w_c, w_o = (w.astype(BF16) for w in (w_conv_out, w_attn_out, w_mem_out, w_out))
    w_up_b, w_dn_b, w_kv_b = w_up.astype(BF16), w_down.astype(BF16), w_mem_kv.astype(BF16)

    bias = _bias_strips(rel_bias)
    kv = _mem_kv(mem, row3(g_mem), w_kv_b)

    xt = x.reshape(TOKENS, D_MODEL)
    for l in range(DEPTH):
        proj, vt = _in_proj(xt, row3(g_pre_mix), w_in_b, w_vt_b, l)
        o = _moba(proj, vt, bias)
        xt = _merge(xt, proj, o, kv, row3(b_gate), conv_mix_w, w_a, w_b, w_c, w_o, row3(g_post_mix), l)
        xt = _ffn(xt, row3(g_pre_ffn), w_up_b, ffn_conv_w, row3(ffn_conv_b), w_dn_b, row3(g_post_ffn), l)
    return xt.reshape(BATCH, SEQ, D_MODEL)
```
